```python
import math
import jax, jax.numpy as jnp
from jax import lax
import numpy as np

D_MODEL = 2048
BATCH = 4
SEQ = 2048
DEPTH = 4
DEC_BATCH = 128
DEC_SEQ = 8
PAST_LEN = 16384
PAGE_SIZE = 128

N_EVEN = (DEPTH + 1) // 2
N_ODD = DEPTH // 2
D_POOL = D_MODEL // 2
POOL_WINDOWS = (2, 4, 8, 16)
N_POOL_GROUPS = len(POOL_WINDOWS)
POOL_GROUP = D_POOL // N_POOL_GROUPS
POOL_BUF = max(POOL_WINDOWS) - 1
D_RNN = D_MODEL // 2
RNN_HEADS = 8
RNN_HEAD_DIM = D_RNN // RNN_HEADS
CONV_W = 4
RG_C = 8.0
D_IN_EVEN = D_POOL + 2 * D_RNN
SSM_GROUP = 16
SSM_GROUPS = D_MODEL // SSM_GROUP
SSM_STATE = 64
SCAN_BLOCK = 128
D_FF = ((8 * D_MODEL // 3 + 127) // 128) * 128
RMS_EPS = 1e-6

kernel_name = "hybrid_pool_rglru_s5_macaron_step"


def rmsnorm(x, g):
    xf = x.astype(jnp.float32)
    y = xf * lax.rsqrt(jnp.mean(xf * xf, axis=-1, keepdims=True) + RMS_EPS)
    return (y * g.astype(jnp.float32)).astype(x.dtype)


def swiglu(x, w_in, w_out):
    gu = x @ w_in
    return (jax.nn.silu(gu[..., :D_FF]) * gu[..., D_FF:]) @ w_out


def pool_mixer(u, buf, pos0, w_grp, scale):
    bsz, L, _ = u.shape
    full = jnp.concatenate([buf, u], axis=1).astype(jnp.float32)
    cs = jnp.pad(jnp.cumsum(full, axis=1), ((0, 0), (1, 0), (0, 0)))
    pos = pos0 + jnp.arange(L)
    end = cs[:, POOL_BUF + 1:]
    means = []
    for g, w in enumerate(POOL_WINDOWS):
        sl = slice(g * POOL_GROUP, (g + 1) * POOL_GROUP)
        start = cs[:, POOL_BUF + 1 - w:POOL_BUF + 1 - w + L, sl]
        cnt = jnp.minimum(pos + 1, w).astype(jnp.float32)[None, :, None]
        means.append((end[..., sl] - start) / cnt)
    diff = (jnp.concatenate(means, axis=-1) - full[:, POOL_BUF:]).astype(u.dtype)
    diff = diff.reshape(bsz, L, N_POOL_GROUPS, POOL_GROUP)
    out = jnp.einsum('blgc,gcd->blgd', diff, w_grp).reshape(bsz, L, D_POOL) * scale
    return out, full[:, -POOL_BUF:].astype(u.dtype)


def causal_depthwise_conv(v, buf, w, b):
    full = jnp.concatenate([buf, v], axis=1)
    y = lax.conv_general_dilated(full, w[:, None, :], window_strides=(1,), padding='VALID',
                                 dimension_numbers=('NWC', 'WIO', 'NWC'),
                                 feature_group_count=v.shape[-1])
    return y + b, full[:, -(CONV_W - 1):]


def block_diag_linear(x, w, b):
    bsz, L, _ = x.shape
    xh = x.reshape(bsz, L, RNN_HEADS, RNN_HEAD_DIM)
    return jnp.einsum('blhi,hij->blhj', xh, w).reshape(bsz, L, D_RNN) + b


def rglru(x, h0, pos0, w_a, b_a, w_x, b_x, lam):
    L = x.shape[1]
    f32 = jnp.float32
    r = jax.nn.sigmoid(block_diag_linear(x, w_a, b_a).astype(f32))
    i = jax.nn.sigmoid(block_diag_linear(x, w_x, b_x).astype(f32))
    log_a = RG_C * r * jax.nn.log_sigmoid(lam.astype(f32))
    reset = (pos0 + jnp.arange(L) == 0)[None, :, None]
    a = jnp.where(reset, 0.0, jnp.exp(log_a))
    mult = jnp.where(reset, 1.0, jnp.sqrt(-jnp.expm1(2.0 * log_a)))
    b = mult * i * x.astype(f32)
    b = b.at[:, 0].add(a[:, 0] * h0.astype(f32))

    def combine(l, rr):
        return (l[0] * rr[0], rr[0] * l[1] + rr[1])

    _, h = lax.associative_scan(combine, (a, b), axis=1)
    return h, h[:, -1]


def even_mixer(hn, pool_buf, conv_buf, h0, pos0, w_in, pool_w, pool_scale, conv_w, conv_b,
               gate_a_w, gate_a_b, gate_x_w, gate_x_b, lam, w_out):
    z = hn @ w_in
    u_pool = z[..., :D_POOL]
    u_rnn = z[..., D_POOL:D_POOL + D_RNN]
    u_gate = z[..., D_POOL + D_RNN:]
    pool_out, new_pool = pool_mixer(u_pool, pool_buf, pos0, pool_w, pool_scale)
    conv_out, new_conv = causal_depthwise_conv(u_rnn, conv_buf, conv_w, conv_b)
    rec, new_h = rglru(conv_out, h0, pos0, gate_a_w, gate_a_b, gate_x_w, gate_x_b, lam)
    rnn_out = rec.astype(hn.dtype) * jax.nn.gelu(u_gate)
    y = jnp.concatenate([pool_out, rnn_out], axis=-1) @ w_out
    return y, new_pool, new_conv, new_h.astype(hn.dtype)


def complex_combine(l, r):
    a1r, a1i, b1r, b1i = l
    a2r, a2i, b2r, b2i = r
    return (a2r * a1r - a2i * a1i, a2r * a1i + a2i * a1r,
            a2r * b1r - a2i * b1i + b2r, a2r * b1i + a2i * b1r + b2i)


def s5_mixer(u, s_re, s_im, lam_re, lam_im, log_step, b_re, b_im, c_re, c_im, d, w_glu):
    bsz, L, _ = u.shape
    f32 = jnp.float32
    lr = jnp.minimum(lam_re.astype(f32), -1e-4)
    li = lam_im.astype(f32)
    dt = jnp.exp(log_step.astype(f32))[:, None]
    mag = jnp.exp(lr * dt)
    ab_re = mag * jnp.cos(li * dt)
    ab_im = mag * jnp.sin(li * dt)
    den = lr * lr + li * li
    nr = ab_re - 1.0
    f_re = (nr * lr + ab_im * li) / den
    f_im = (ab_im * lr - nr * li) / den
    br = b_re.astype(f32)
    bi = b_im.astype(f32)
    bb_re = f_re[..., None] * br - f_im[..., None] * bi
    bb_im = f_re[..., None] * bi + f_im[..., None] * br
    cr = c_re.astype(f32)
    ci = c_im.astype(f32)
    dg = d.astype(f32).reshape(SSM_GROUPS, SSM_GROUP)
    blk = math.gcd(L, SCAN_BLOCK)
    n_blk = L // blk
    ub = u.astype(f32).reshape(bsz, n_blk, blk, SSM_GROUPS, SSM_GROUP).transpose(1, 2, 0, 3, 4)
    a_re = jnp.broadcast_to(ab_re, (blk, bsz, SSM_GROUPS, SSM_STATE))
    a_im = jnp.broadcast_to(ab_im, (blk, bsz, SSM_GROUPS, SSM_STATE))

    def step(carry, ublk):
        xr0, xi0 = carry
        bu_re = jnp.einsum('tbgc,gpc->tbgp', ublk, bb_re)
        bu_im = jnp.einsum('tbgc,gpc->tbgp', ublk, bb_im)
        bu_re = bu_re.at[0].add(ab_re * xr0 - ab_im * xi0)
        bu_im = bu_im.at[0].add(ab_re * xi0 + ab_im * xr0)
        _, _, xr, xi = lax.associative_scan(complex_combine, (a_re, a_im, bu_re, bu_im), axis=0)
        y = (jnp.einsum('tbgp,gcp->tbgc', xr, cr) - jnp.einsum('tbgp,gcp->tbgc', xi, ci)
             + dg * ublk)
        return (xr[-1], xi[-1]), y

    (sr, si), yb = lax.scan(step, (s_re.astype(f32), s_im.astype(f32)), ub)
    y = yb.transpose(2, 0, 1, 3, 4).reshape(bsz, L, D_MODEL)
    z = jax.nn.gelu(y).astype(u.dtype) @ w_glu
    out = z[..., :D_MODEL] * jax.nn.sigmoid(z[..., D_MODEL:])
    return out, sr.astype(u.dtype), si.astype(u.dtype)


def trunk(x, st_pool, st_conv, st_h, st_re, st_im, pos0, p):
    new_pool, new_conv, new_h, new_re, new_im = [], [], [], [], []
    for layer in range(DEPTH):
        x = x + 0.5 * swiglu(rmsnorm(x, p['norm_ffn1'][layer]), p['w_ffn1_in'][layer], p['w_ffn1_out'][layer])
        hn = rmsnorm(x, p['norm_mix'][layer])
        if layer % 2 == 0:
            e = layer // 2
            y, pb, cb, hh = even_mixer(hn, st_pool[e], st_conv[e], st_h[e], pos0,
                                       p['w_in_even'][e], p['pool_w'][e], p['pool_scale'][e],
                                       p['conv_w'][e], p['conv_b'][e], p['gate_a_w'][e], p['gate_a_b'][e],
                                       p['gate_x_w'][e], p['gate_x_b'][e], p['rglru_lambda'][e],
                                       p['w_out_even'][e])
            new_pool.append(pb)
            new_conv.append(cb)
            new_h.append(hh)
        else:
            o = layer // 2
            y, sr, si = s5_mixer(hn, st_re[o], st_im[o], p['ssm_lambda_re'][o], p['ssm_lambda_im'][o],
                                 p['ssm_log_step'][o], p['ssm_b_re'][o], p['ssm_b_im'][o],
                                 p['ssm_c_re'][o], p['ssm_c_im'][o], p['ssm_d'][o], p['w_glu'][o])
            new_re.append(sr)
            new_im.append(si)
        x = x + y
        x = x + 0.5 * swiglu(rmsnorm(x, p['norm_ffn2'][layer]), p['w_ffn2_in'][layer], p['w_ffn2_out'][layer])
    return (rmsnorm(x, p['final_norm']), jnp.stack(new_pool), jnp.stack(new_conv), jnp.stack(new_h),
            jnp.stack(new_re), jnp.stack(new_im))


def setup_inputs(seed: int = 0) -> dict:
    key = jax.random.key(seed)
    ks = iter(jax.random.split(key, 40))
    f32 = jnp.float32

    def nrm(shape, s):
        return jax.random.normal(next(ks), shape, f32) * s

    rad = jnp.sqrt(jax.random.uniform(next(ks), (N_EVEN, D_RNN), f32, 0.81, 0.998))
    lam_im0 = jnp.broadcast_to(math.pi * jnp.arange(SSM_STATE, dtype=f32), (N_ODD, SSM_GROUPS, SSM_STATE))
    return {
        "x_prompt": nrm((BATCH, SEQ, D_MODEL), 1.0),
        "x_sample": nrm((DEC_BATCH, DEC_SEQ, D_MODEL), 1.0),
        "state_pool": nrm((N_EVEN, DEC_BATCH, POOL_BUF, D_POOL), 1.0),
        "state_conv": nrm((N_EVEN, DEC_BATCH, CONV_W - 1, D_RNN), 1.0),
        "state_rglru": nrm((N_EVEN, DEC_BATCH, D_RNN), 0.5),
        "state_ssm_re": nrm((N_ODD, DEC_BATCH, SSM_GROUPS, SSM_STATE), 0.1),
        "state_ssm_im": nrm((N_ODD, DEC_BATCH, SSM_GROUPS, SSM_STATE), 0.1),
        "norm_ffn1": 1.0 + nrm((DEPTH, D_MODEL), 0.02),
        "w_ffn1_in": nrm((DEPTH, D_MODEL, 2 * D_FF), D_MODEL ** -0.5),
        "w_ffn1_out": nrm((DEPTH, D_FF, D_MODEL), D_FF ** -0.5),
        "norm_mix": 1.0 + nrm((DEPTH, D_MODEL), 0.02),
        "norm_ffn2": 1.0 + nrm((DEPTH, D_MODEL), 0.02),
        "w_ffn2_in": nrm((DEPTH, D_MODEL, 2 * D_FF), D_MODEL ** -0.5),
        "w_ffn2_out": nrm((DEPTH, D_FF, D_MODEL), D_FF ** -0.5),
        "w_in_even": nrm((N_EVEN, D_MODEL, D_IN_EVEN), D_MODEL ** -0.5),
        "pool_w": nrm((N_EVEN, N_POOL_GROUPS, POOL_GROUP, POOL_GROUP), POOL_GROUP ** -0.5),
        "pool_scale": 1.0 + nrm((N_EVEN, D_POOL), 0.02),
        "conv_w": nrm((N_EVEN, CONV_W, D_RNN), CONV_W ** -0.5),
        "conv_b": nrm((N_EVEN, D_RNN), 0.01),
        "gate_a_w": nrm((N_EVEN, RNN_HEADS, RNN_HEAD_DIM, RNN_HEAD_DIM), RNN_HEAD_DIM ** -0.5),
        "gate_a_b": nrm((N_EVEN, D_RNN), 0.01),
        "gate_x_w": nrm((N_EVEN, RNN_HEADS, RNN_HEAD_DIM, RNN_HEAD_DIM), RNN_HEAD_DIM ** -0.5),
        "gate_x_b": nrm((N_EVEN, D_RNN), 0.01),
        "rglru_lambda": jnp.log(rad) - jnp.log1p(-rad),
        "w_out_even": nrm((N_EVEN, D_POOL + D_RNN, D_MODEL), (D_POOL + D_RNN) ** -0.5),
        "ssm_lambda_re": -0.5 + nrm((N_ODD, SSM_GROUPS, SSM_STATE), 0.01),
        "ssm_lambda_im": lam_im0 + nrm((N_ODD, SSM_GROUPS, SSM_STATE), 0.01),
        "ssm_log_step": jax.random.uniform(next(ks), (N_ODD, SSM_GROUPS), f32, math.log(1e-3), math.log(1e-1)),
        "ssm_b_re": nrm((N_ODD, SSM_GROUPS, SSM_STATE, SSM_GROUP), (2 * SSM_GROUP) ** -0.5),
        "ssm_b_im": nrm((N_ODD, SSM_GROUPS, SSM_STATE, SSM_GROUP), (2 * SSM_GROUP) ** -0.5),
        "ssm_c_re": nrm((N_ODD, SSM_GROUPS, SSM_GROUP, SSM_STATE), SSM_STATE ** -0.5),
        "ssm_c_im": nrm((N_ODD, SSM_GROUPS, SSM_GROUP, SSM_STATE), SSM_STATE ** -0.5),
        "ssm_d": nrm((N_ODD, D_MODEL), 1.0),
        "w_glu": nrm((N_ODD, D_MODEL, 2 * D_MODEL), D_MODEL ** -0.5),
        "final_norm": 1.0 + nrm((D_MODEL,), 0.02),
    }


def reference(x_prompt, x_sample, state_pool, state_conv, state_rglru, state_ssm_re, state_ssm_im,
              norm_ffn1, w_ffn1_in, w_ffn1_out, norm_mix, norm_ffn2, w_ffn2_in, w_ffn2_out,
              w_in_even, pool_w, pool_scale, conv_w, conv_b, gate_a_w, gate_a_b, gate_x_w, gate_x_b,
              rglru_lambda, w_out_even, ssm_lambda_re, ssm_lambda_im, ssm_log_step, ssm_b_re, ssm_b_im,
              ssm_c_re, ssm_c_im, ssm_d, w_glu, final_norm):
    p = dict(norm_ffn1=norm_ffn1, w_ffn1_in=w_ffn1_in, w_ffn1_out=w_ffn1_out, norm_mix=norm_mix,
             norm_ffn2=norm_ffn2, w_ffn2_in=w_ffn2_in, w_ffn2_out=w_ffn2_out, w_in_even=w_in_even,
             pool_w=pool_w, pool_scale=pool_scale, conv_w=conv_w, conv_b=conv_b, gate_a_w=gate_a_w,
             gate_a_b=gate_a_b, gate_x_w=gate_x_w, gate_x_b=gate_x_b, rglru_lambda=rglru_lambda,
             w_out_even=w_out_even, ssm_lambda_re=ssm_lambda_re, ssm_lambda_im=ssm_lambda_im,
             ssm_log_step=ssm_log_step, ssm_b_re=ssm_b_re, ssm_b_im=ssm_b_im, ssm_c_re=ssm_c_re,
             ssm_c_im=ssm_c_im, ssm_d=ssm_d, w_glu=w_glu, final_norm=final_norm)
    dt = x_prompt.dtype
    y_prompt, pool_p, conv_p, h_p, re_p, im_p = trunk(
        x_prompt,
        jnp.zeros((N_EVEN, BATCH, POOL_BUF, D_POOL), dt),
        jnp.zeros((N_EVEN, BATCH, CONV_W - 1, D_RNN), dt),
        jnp.zeros((N_EVEN, BATCH, D_RNN), dt),
        jnp.zeros((N_ODD, BATCH, SSM_GROUPS, SSM_STATE), dt),
        jnp.zeros((N_ODD, BATCH, SSM_GROUPS, SSM_STATE), dt),
        0, p)
    y_sample, pool_s, conv_s, h_s, re_s, im_s = trunk(
        x_sample, state_pool, state_conv, state_rglru, state_ssm_re, state_ssm_im, PAST_LEN, p)
    return (y_prompt, y_sample, pool_p, pool_s, conv_p, conv_s, h_p, h_s, re_p, re_s, im_p, im_s)
```

```python
import functools
import math

import jax
import jax.numpy as jnp
from jax import lax
from jax.experimental import pallas as pl
from jax.experimental.pallas import tpu as pltpu

D_MODEL = 2048
BATCH = 4
SEQ = 2048
DEPTH = 4
DEC_BATCH = 128
DEC_SEQ = 8
PAST_LEN = 16384
N_EVEN = (DEPTH + 1) // 2
N_ODD = DEPTH // 2
D_POOL = D_MODEL // 2
POOL_WINDOWS = (2, 4, 8, 16)
POOL_GROUP = D_POOL // len(POOL_WINDOWS)
POOL_BUF = max(POOL_WINDOWS) - 1
D_RNN = D_MODEL // 2
RNN_HEADS = 8
RNN_HEAD_DIM = D_RNN // RNN_HEADS
CONV_W = 4
RG_C = 8.0
SSM_GROUP = 16
SSM_GROUPS = D_MODEL // SSM_GROUP
SSM_STATE = 64
D_FF = ((8 * D_MODEL // 3 + 127) // 128) * 128
RMS_EPS = 1e-6

F32 = jnp.float32
BF16 = jnp.bfloat16

SUBLANES = 8
MP = BATCH * SEQ
MS = DEC_BATCH * DEC_SEQ
M = MP + MS
ROW_TILE = 1024
N_PROMPT_TILES = MP // ROW_TILE
N_TILES = M // ROW_TILE
PROMPT_TT = ROW_TILE // BATCH
assert MS == ROW_TILE and N_TILES == N_PROMPT_TILES + 1

FF_TILE = 512
D_FF_PAD = ((D_FF + FF_TILE - 1) // FF_TILE) * FF_TILE
FFN_TM = 512
MM_TM = 512
MM_TN = 512

CH_BLK = POOL_GROUP
HEADS_PER_BLK = CH_BLK // RNN_HEAD_DIM
SSM_BLK_GROUPS = 16
SSM_BLK_IN = SSM_BLK_GROUPS * SSM_GROUP
SSM_BLK_ST = SSM_BLK_GROUPS * SSM_STATE
N_SSM_BLK = SSM_GROUPS // SSM_BLK_GROUPS
SCAN_LANES = 512

VMEM_LIMIT = 48 * 1024 * 1024
FFN_VMEM_LIMIT = 56 * 1024 * 1024


def _round_up(x, m):
    return (x + m - 1) // m * m


def _sigmoid(x):
    return 1.0 / (1.0 + jnp.exp(-x))


def _gelu_tanh(x):
    c = math.sqrt(2.0 / math.pi)
    return 0.5 * x * (1.0 + jnp.tanh(c * (x + 0.044715 * (x * x * x))))


def _rms(x, g):
    ms = jnp.mean(x * x, axis=-1, keepdims=True)
    return x * lax.rsqrt(ms + RMS_EPS) * g


def _ffn_body(x_ref, g1_ref, wg_ref, wu_ref, wo_ref, g2_ref, *rest, nf, emit_x, emit_norm):
    outs, (xn_scr, acc_scr) = rest[:-2], rest[-2:]
    xo_ref = outs[0] if emit_x else None
    hn_ref = outs[-1] if emit_norm else None
    j = pl.program_id(1)

    @pl.when(j == 0)
    def _():
        xn_scr[...] = _rms(x_ref[...], g1_ref[...]).astype(BF16)
        acc_scr[...] = jnp.zeros_like(acc_scr)

    xn = xn_scr[...]
    hg = jnp.dot(xn, wg_ref[...], preferred_element_type=F32)
    hu = jnp.dot(xn, wu_ref[...], preferred_element_type=F32)
    act = (hg * _sigmoid(hg) * hu).astype(BF16)
    acc_scr[...] += jnp.dot(act, wo_ref[...], preferred_element_type=F32)

    @pl.when(j == nf - 1)
    def _():
        xnew = x_ref[...] + 0.5 * acc_scr[...]
        if emit_x:
            xo_ref[...] = xnew
        if emit_norm:
            hn_ref[...] = _rms(xnew, g2_ref[...])


def _ffn(x, g1, wg, wu, wo, g2, layer, *, emit_x=True, emit_norm=True):
    nf = D_FF_PAD // FF_TILE
    row = lambda i, j: (i, 0)
    n_out = int(emit_x) + int(emit_norm)
    return pl.pallas_call(
        functools.partial(_ffn_body, nf=nf, emit_x=emit_x, emit_norm=emit_norm),
        out_shape=tuple(jax.ShapeDtypeStruct((M, D_MODEL), F32) for _ in range(n_out)),
        grid=(M // FFN_TM, nf),
        in_specs=[
            pl.BlockSpec((FFN_TM, D_MODEL), row),
            pl.BlockSpec((1, D_MODEL), lambda i, j: (0, 0)),
            pl.BlockSpec((None, D_MODEL, FF_TILE), lambda i, j: (layer, 0, j)),
            pl.BlockSpec((None, D_MODEL, FF_TILE), lambda i, j: (layer, 0, j)),
            pl.BlockSpec((None, FF_TILE, D_MODEL), lambda i, j: (layer, j, 0)),
            pl.BlockSpec((1, D_MODEL), lambda i, j: (0, 0)),
        ],
        out_specs=tuple(pl.BlockSpec((FFN_TM, D_MODEL), row) for _ in range(n_out)),
        scratch_shapes=[pltpu.VMEM((FFN_TM, D_MODEL), BF16), pltpu.VMEM((FFN_TM, D_MODEL), F32)],
        compiler_params=pltpu.CompilerParams(
            dimension_semantics=("arbitrary", "arbitrary"), vmem_limit_bytes=FFN_VMEM_LIMIT),
        name="ffn",
    )(x, g1, wg, wu, wo, g2)


def _inproj_body(a_ref, w_ref, o_ref):
    o_ref[...] = jnp.dot(a_ref[...].astype(BF16), w_ref[...], preferred_element_type=F32)


def _inproj(a, w, layer):
    n = w.shape[-1]
    return pl.pallas_call(
        _inproj_body,
        out_shape=jax.ShapeDtypeStruct((M, n), F32),
        grid=(M // MM_TM, n // MM_TN),
        in_specs=[
            pl.BlockSpec((MM_TM, D_MODEL), lambda i, j: (i, 0)),
            pl.BlockSpec((None, D_MODEL, MM_TN), lambda i, j: (layer, 0, j)),
        ],
        out_specs=pl.BlockSpec((MM_TM, MM_TN), lambda i, j: (i, j)),
        compiler_params=pltpu.CompilerParams(
            dimension_semantics=("arbitrary", "arbitrary"), vmem_limit_bytes=VMEM_LIMIT),
        name="even_inproj",
    )(a, w)


def _outproj_body(x_ref, p_ref, r_ref, wp_ref, wr_ref, o_ref):
    o_ref[...] = (x_ref[...] + jnp.dot(p_ref[...], wp_ref[...], preferred_element_type=F32)
                  + jnp.dot(r_ref[...], wr_ref[...], preferred_element_type=F32))


def _outproj(x, p, r, w, layer):
    return pl.pallas_call(
        _outproj_body,
        out_shape=jax.ShapeDtypeStruct((M, D_MODEL), F32),
        grid=(M // MM_TM, D_MODEL // MM_TN),
        in_specs=[
            pl.BlockSpec((MM_TM, MM_TN), lambda i, j: (i, j)),
            pl.BlockSpec((MM_TM, D_POOL), lambda i, j: (i, 0)),
            pl.BlockSpec((MM_TM, D_RNN), lambda i, j: (i, 0)),
            pl.BlockSpec((None, D_POOL, MM_TN), lambda i, j: (layer, 0, j)),
            pl.BlockSpec((None, D_RNN, MM_TN), lambda i, j: (layer, 1, j)),
        ],
        out_specs=pl.BlockSpec((MM_TM, MM_TN), lambda i, j: (i, j)),
        compiler_params=pltpu.CompilerParams(
            dimension_semantics=("arbitrary", "arbitrary"), vmem_limit_bytes=VMEM_LIMIT),
        name="even_outproj",
    )(x, p, r, w, w)


def _glu_body(x_ref, a_ref, wa_ref, wb_ref, o_ref):
    a = a_ref[...]
    za = jnp.dot(a, wa_ref[...], preferred_element_type=F32)
    zb = jnp.dot(a, wb_ref[...], preferred_element_type=F32)
    o_ref[...] = x_ref[...] + za * _sigmoid(zb)


def _glu(x, a, w, layer):
    nj = D_MODEL // MM_TN
    return pl.pallas_call(
        _glu_body,
        out_shape=jax.ShapeDtypeStruct((M, D_MODEL), F32),
        grid=(M // MM_TM, nj),
        in_specs=[
            pl.BlockSpec((MM_TM, MM_TN), lambda i, j: (i, j)),
            pl.BlockSpec((MM_TM, D_MODEL), lambda i, j: (i, 0)),
            pl.BlockSpec((None, D_MODEL, MM_TN), lambda i, j: (layer, 0, j)),
            pl.BlockSpec((None, D_MODEL, MM_TN), lambda i, j: (layer, 0, nj + j)),
        ],
        out_specs=pl.BlockSpec((MM_TM, MM_TN), lambda i, j: (i, j)),
        compiler_params=pltpu.CompilerParams(
            dimension_semantics=("arbitrary", "arbitrary"), vmem_limit_bytes=VMEM_LIMIT),
        name="s5_glu",
    )(x, a, w, w)


def _hi_half_mask(width):
    return lax.broadcasted_iota(jnp.int32, (SUBLANES, width), 0) >= SUBLANES // 2


def _even_path(c, first, last, up_ref, ur_ref, ug_ref, pw_ref, ps_ref, cw_ref, cb_ref, wa_ref, ba_ref,
               wx_ref, bx_ref, lam_ref, init_pool, init_conv, init_h, pout_ref, rout_ref, npool_ref,
               nconv_ref, nh_ref, pfull, cfull, a_scr, b_scr, d_scr, h_scr, *, nb, tt, t0, pos0, carry):
    rows = nb * tt
    ph, pp = POOL_BUF * nb, _round_up(POOL_BUF * nb, SUBLANES)
    chh, cp = (CONV_W - 1) * nb, _round_up((CONV_W - 1) * nb, SUBLANES)
    shift = int(math.log2(nb))

    @pl.when(first)
    def _():
        pfull[pp - ph:pp, :] = init_pool()
        cfull[cp - chh:cp, :] = init_conv()
        h0 = init_h()
        if nb < SUBLANES:
            h_scr[...] = jnp.concatenate([h0, h0], axis=0)
        else:
            h_scr[...] = h0

    pfull[pp:pp + rows, :] = up_ref[...]
    cfull[cp:cp + rows, :] = ur_ref[...]

    row_t = t0 + lax.shift_right_logical(lax.broadcasted_iota(jnp.int32, (rows, 1), 0), shift)

    for g, w in enumerate(POOL_WINDOWS):
        @pl.when(c == g)
        def _(w=w):
            cur = pfull[pp:pp + rows, :]
            s = cur
            for k in range(1, w):
                s = s + pfull[pp - k * nb:pp - k * nb + rows, :]
            cnt = jnp.minimum(pos0 + row_t + 1, w).astype(F32)
            d_scr[...] = s / cnt - cur

    pool = jnp.dot(d_scr[...].astype(BF16), pw_ref[...], preferred_element_type=F32) * ps_ref[...]
    pout_ref[...] = pool.astype(BF16)

    y = cb_ref[...] + cw_ref[CONV_W - 1:CONV_W, :] * cfull[cp:cp + rows, :]
    for k in range(CONV_W - 1):
        off = cp - (CONV_W - 1 - k) * nb
        y = y + cw_ref[k:k + 1, :] * cfull[off:off + rows, :]

    yb = y.astype(BF16)
    gr, gi = [], []
    for hd in range(HEADS_PER_BLK):
        xh = yb[:, hd * RNN_HEAD_DIM:(hd + 1) * RNN_HEAD_DIM]
        gr.append(jnp.dot(xh, wa_ref[hd], preferred_element_type=F32))
        gi.append(jnp.dot(xh, wx_ref[hd], preferred_element_type=F32))
    r = _sigmoid(jnp.concatenate(gr, axis=-1) + ba_ref[...])
    ig = _sigmoid(jnp.concatenate(gi, axis=-1) + bx_ref[...])
    lam = lam_ref[...]
    log_sig = jnp.minimum(lam, 0.0) - jnp.log(1.0 + jnp.exp(-jnp.abs(lam)))
    log_a = RG_C * r * log_sig
    a = jnp.exp(log_a)
    mult = jnp.sqrt(1.0 - jnp.exp(2.0 * log_a))
    if pos0 == 0:
        reset = row_t == 0
        a = jnp.where(reset, 0.0, a)
        mult = jnp.where(reset, 1.0, mult)
    a_scr[...] = a
    b_scr[...] = mult * ig * y

    if nb < SUBLANES:
        hi = _hi_half_mask(CH_BLK)
        half = SUBLANES // 2

        def step(k, hprev):
            rs = pl.ds(pl.multiple_of(k * SUBLANES, SUBLANES), SUBLANES)
            av = a_scr[rs, :]
            bv = b_scr[rs, :]
            b1 = jnp.where(hi, bv + av * pltpu.roll(bv, half, 0), bv)
            a1 = jnp.where(hi, av * pltpu.roll(av, half, 0), av)
            cprev = jnp.where(hi, hprev, pltpu.roll(hprev, half, 0))
            hv = b1 + a1 * cprev
            b_scr[rs, :] = hv
            return hv

        hlast = lax.fori_loop(0, rows // SUBLANES, step, h_scr[...])
        h_scr[...] = hlast
        nh_ref[...] = hlast[half:, :]
    else:
        hv = h_scr[...]
        for t in range(tt):
            hv = a_scr[t * nb:(t + 1) * nb, :] * hv + b_scr[t * nb:(t + 1) * nb, :]
            b_scr[t * nb:(t + 1) * nb, :] = hv
        h_scr[...] = hv
        nh_ref[...] = hv

    rout_ref[...] = (b_scr[...] * _gelu_tanh(ug_ref[...])).astype(BF16)

    new_pool = pfull[pp + rows - ph:pp + rows, :]
    new_conv = cfull[cp + rows - chh:cp + rows, :]
    npool_ref[...] = new_pool
    nconv_ref[...] = new_conv
    if carry:
        @pl.when(jnp.logical_not(last))
        def _():
            pfull[pp - ph:pp, :] = new_pool
            cfull[cp - chh:cp, :] = new_conv


def _even_body(up_ref, ur_ref, ug_ref, pw_ref, ps_ref, cw_ref, cb_ref, wa_ref, ba_ref, wx_ref, bx_ref,
               lam_ref, spool_ref, sconv_ref, sh_ref,
               pout_ref, rout_ref, npool_p, npool_s, nconv_p, nconv_s, nh_p, nh_s,
               pfull_p, cfull_p, pfull_s, cfull_s, a_scr, b_scr, d_scr, h_p, h_s):
    c = pl.program_id(0)
    i = pl.program_id(1)
    common = (up_ref, ur_ref, ug_ref, pw_ref, ps_ref, cw_ref, cb_ref, wa_ref, ba_ref, wx_ref, bx_ref, lam_ref)

    @pl.when(i < N_PROMPT_TILES)
    def _():
        _even_path(c, i == 0, i == N_PROMPT_TILES - 1, *common,
                   lambda: jnp.zeros((POOL_BUF * BATCH, CH_BLK), F32),
                   lambda: jnp.zeros(((CONV_W - 1) * BATCH, CH_BLK), F32),
                   lambda: jnp.zeros((BATCH, CH_BLK), F32),
                   pout_ref, rout_ref, npool_p, nconv_p, nh_p, pfull_p, cfull_p, a_scr, b_scr, d_scr, h_p,
                   nb=BATCH, tt=PROMPT_TT, t0=i * PROMPT_TT, pos0=0, carry=True)

    @pl.when(i == N_PROMPT_TILES)
    def _():
        _even_path(c, True, True, *common,
                   lambda: spool_ref[...], lambda: sconv_ref[...], lambda: sh_ref[...],
                   pout_ref, rout_ref, npool_s, nconv_s, nh_s, pfull_s, cfull_s, a_scr, b_scr, d_scr, h_s,
                   nb=DEC_BATCH, tt=DEC_SEQ, t0=0, pos0=PAST_LEN, carry=False)


def _even_mix(z, pw, ps, cw, cb, wa, ba, wx, bx, lam, spool, sconv, sh, layer):
    e = layer
    ncb = D_POOL // CH_BLK
    zspec = lambda part: pl.BlockSpec((ROW_TILE, CH_BLK), lambda c, i: (i, part * ncb + c))
    vec = lambda: pl.BlockSpec((None, 1, CH_BLK), lambda c, i: (e, 0, c))
    heads = lambda: pl.BlockSpec((None, HEADS_PER_BLK, RNN_HEAD_DIM, RNN_HEAD_DIM), lambda c, i: (e, c, 0, 0))
    st = lambda n: pl.BlockSpec((None, n, CH_BLK), lambda c, i: (e, 0, c))
    st_shape = lambda n: jax.ShapeDtypeStruct((N_EVEN, n, D_POOL), F32)
    php, phs = POOL_BUF * BATCH, POOL_BUF * DEC_BATCH
    chp, chs = (CONV_W - 1) * BATCH, (CONV_W - 1) * DEC_BATCH
    return pl.pallas_call(
        _even_body,
        out_shape=(
            jax.ShapeDtypeStruct((M, D_POOL), BF16), jax.ShapeDtypeStruct((M, D_RNN), BF16),
            jax.ShapeDtypeStruct((php, D_POOL), F32), jax.ShapeDtypeStruct((phs, D_POOL), F32),
            jax.ShapeDtypeStruct((chp, D_RNN), F32), jax.ShapeDtypeStruct((chs, D_RNN), F32),
            jax.ShapeDtypeStruct((BATCH, D_RNN), F32), jax.ShapeDtypeStruct((DEC_BATCH, D_RNN), F32),
        ),
        grid=(ncb, N_TILES),
        in_specs=[
            zspec(0), zspec(1), zspec(2),
            pl.BlockSpec((None, None, POOL_GROUP, POOL_GROUP), lambda c, i: (e, c, 0, 0)),
            vec(),
            pl.BlockSpec((None, CONV_W, CH_BLK), lambda c, i: (e, 0, c)),
            vec(), heads(), vec(), heads(), vec(), vec(),
            st(phs), st(chs), st(DEC_BATCH),
        ],
        out_specs=(
            pl.BlockSpec((ROW_TILE, CH_BLK), lambda c, i: (i, c)),
            pl.BlockSpec((ROW_TILE, CH_BLK), lambda c, i: (i, c)),
            pl.BlockSpec((php, CH_BLK), lambda c, i: (0, c)),
            pl.BlockSpec((phs, CH_BLK), lambda c, i: (0, c)),
            pl.BlockSpec((chp, CH_BLK), lambda c, i: (0, c)),
            pl.BlockSpec((chs, CH_BLK), lambda c, i: (0, c)),
            pl.BlockSpec((BATCH, CH_BLK), lambda c, i: (0, c)),
            pl.BlockSpec((DEC_BATCH, CH_BLK), lambda c, i: (0, c)),
        ),
        scratch_shapes=[
            pltpu.VMEM((_round_up(php, SUBLANES) + ROW_TILE, CH_BLK), F32),
            pltpu.VMEM((_round_up(chp, SUBLANES) + ROW_TILE, CH_BLK), F32),
            pltpu.VMEM((_round_up(phs, SUBLANES) + ROW_TILE, CH_BLK), F32),
            pltpu.VMEM((_round_up(chs, SUBLANES) + ROW_TILE, CH_BLK), F32),
            pltpu.VMEM((ROW_TILE, CH_BLK), F32),
            pltpu.VMEM((ROW_TILE, CH_BLK), F32),
            pltpu.VMEM((ROW_TILE, CH_BLK), F32),
            pltpu.VMEM((SUBLANES, CH_BLK), F32),
            pltpu.VMEM((DEC_BATCH, CH_BLK), F32),
        ],
        compiler_params=pltpu.CompilerParams(
            dimension_semantics=("arbitrary", "arbitrary"), vmem_limit_bytes=VMEM_LIMIT),
        name="even_mix",
    )(z, z, z, pw, ps, cw, cb, wa, ba, wx, bx, lam, spool, sconv, sh)


def _s5_param_body(lr_ref, li_ref, ls_ref, bre_ref, bim_ref, ab_ref, bb_ref):
    lr = jnp.minimum(lr_ref[...], -1e-4)
    li = li_ref[...]
    dt = jnp.exp(ls_ref[...])
    mag = jnp.exp(lr * dt)
    ab_re = mag * jnp.cos(li * dt)
    ab_im = mag * jnp.sin(li * dt)
    den = lr * lr + li * li
    nr = ab_re - 1.0
    f_re = (nr * lr + ab_im * li) / den
    f_im = (ab_im * lr - nr * li) / den
    ab_ref[0:1, :] = ab_re
    ab_ref[1:2, :] = ab_im
    bre = bre_ref[...]
    bim = bim_ref[...]
    bb_ref[:, :SSM_BLK_ST] = (f_re * bre - f_im * bim).astype(BF16)
    bb_ref[:, SSM_BLK_ST:] = (f_re * bim + f_im * bre).astype(BF16)


def _s5_params(lr, li, ls, bre, bim):
    row = pl.BlockSpec((None, 1, SSM_BLK_ST), lambda k: (k, 0, 0))
    mat = pl.BlockSpec((None, SSM_BLK_IN, SSM_BLK_ST), lambda k: (k, 0, 0))
    return pl.pallas_call(
        _s5_param_body,
        out_shape=(jax.ShapeDtypeStruct((N_SSM_BLK, 2, SSM_BLK_ST), F32),
                   jax.ShapeDtypeStruct((N_SSM_BLK, SSM_BLK_IN, 2 * SSM_BLK_ST), BF16)),
        grid=(N_SSM_BLK,),
        in_specs=[row, row, row, mat, mat],
        out_specs=(pl.BlockSpec((None, 2, SSM_BLK_ST), lambda k: (k, 0, 0)),
                   pl.BlockSpec((None, SSM_BLK_IN, 2 * SSM_BLK_ST), lambda k: (k, 0, 0))),
        compiler_params=pltpu.CompilerParams(dimension_semantics=("arbitrary",)),
        name="s5_params",
    )(lr, li, ls, bre, bim)


def _s5_path(first, u_ref, bb_ref, ab_ref, cre_ref, cim_ref, d_ref, init_re, init_im,
             g_ref, nre_ref, nim_ref, xr_scr, xi_scr, st_re, st_im, *, nb, tt):
    rows = nb * tt
    half = SUBLANES // 2

    @pl.when(first)
    def _():
        s_re, s_im = init_re(), init_im()
        if nb < SUBLANES:
            st_re[...] = jnp.concatenate([s_re, s_re], axis=0)
            st_im[...] = jnp.concatenate([s_im, s_im], axis=0)
        else:
            st_re[...] = s_re
            st_im[...] = s_im

    u = u_ref[...]
    bu = jnp.dot(u.astype(BF16), bb_ref[...], preferred_element_type=F32)
    xr_scr[...] = bu[:, :SSM_BLK_ST]
    xi_scr[...] = bu[:, SSM_BLK_ST:]

    for lc in range(SSM_BLK_ST // SCAN_LANES):
        ls = slice(lc * SCAN_LANES, (lc + 1) * SCAN_LANES)
        ar = jnp.broadcast_to(ab_ref[0:1, ls], (SUBLANES, SCAN_LANES))
        ai = jnp.broadcast_to(ab_ref[1:2, ls], (SUBLANES, SCAN_LANES))
        if nb < SUBLANES:
            hi = _hi_half_mask(SCAN_LANES)
            m_re = jnp.where(hi, ar, 0.0)
            m_im = jnp.where(hi, ai, 0.0)
            p_re = jnp.where(hi, ar * ar - ai * ai, ar)
            p_im = jnp.where(hi, 2.0 * ar * ai, ai)

            def step(k, carry, ls=ls, m_re=m_re, m_im=m_im, p_re=p_re, p_im=p_im, hi=hi):
                cr, ci = carry
                rs = pl.ds(pl.multiple_of(k * SUBLANES, SUBLANES), SUBLANES)
                br = xr_scr[rs, ls]
                bi = xi_scr[rs, ls]
                br_s = pltpu.roll(br, half, 0)
                bi_s = pltpu.roll(bi, half, 0)
                wr = br + (m_re * br_s - m_im * bi_s)
                wi = bi + (m_re * bi_s + m_im * br_s)
                pr = jnp.where(hi, cr, pltpu.roll(cr, half, 0))
                pi = jnp.where(hi, ci, pltpu.roll(ci, half, 0))
                xr = wr + (p_re * pr - p_im * pi)
                xi = wi + (p_re * pi + p_im * pr)
                xr_scr[rs, ls] = xr
                xi_scr[rs, ls] = xi
                return xr, xi

            cr, ci = lax.fori_loop(0, rows // SUBLANES, step, (st_re[:, ls], st_im[:, ls]))
            st_re[:, ls] = cr
            st_im[:, ls] = ci
        else:
            def chunk(q, _, ls=ls, ar=ar, ai=ai):
                r0 = pl.multiple_of(q * SUBLANES, SUBLANES)
                xr = st_re[pl.ds(r0, SUBLANES), ls]
                xi = st_im[pl.ds(r0, SUBLANES), ls]
                for t in range(tt):
                    rs = pl.ds(pl.multiple_of(t * nb + r0, SUBLANES), SUBLANES)
                    nr = ar * xr - ai * xi + xr_scr[rs, ls]
                    ni = ar * xi + ai * xr + xi_scr[rs, ls]
                    xr_scr[rs, ls] = nr
                    xi_scr[rs, ls] = ni
                    xr, xi = nr, ni
                st_re[pl.ds(r0, SUBLANES), ls] = xr
                st_im[pl.ds(r0, SUBLANES), ls] = xi
                return 0

            lax.fori_loop(0, nb // SUBLANES, chunk, 0)

    y = (jnp.dot(xr_scr[...].astype(BF16), cre_ref[...], preferred_element_type=F32)
         - jnp.dot(xi_scr[...].astype(BF16), cim_ref[...], preferred_element_type=F32)
         + d_ref[...] * u)
    g_ref[...] = _gelu_tanh(y).astype(BF16)
    if nb < SUBLANES:
        nre_ref[...] = st_re[half:, :]
        nim_ref[...] = st_im[half:, :]
    else:
        nre_ref[...] = st_re[...]
        nim_ref[...] = st_im[...]


def _s5_body(u_ref, bb_ref, ab_ref, cre_ref, cim_ref, d_ref, sre_ref, sim_ref,
             g_ref, nre_p, nim_p, nre_s, nim_s, xr_scr, xi_scr, re_p, im_p, re_s, im_s):
    i = pl.program_id(1)
    common = (u_ref, bb_ref, ab_ref, cre_ref, cim_ref, d_ref)
    zeros = lambda: jnp.zeros((BATCH, SSM_BLK_ST), F32)

    @pl.when(i < N_PROMPT_TILES)
    def _():
        _s5_path(i == 0, *common, zeros, zeros, g_ref, nre_p, nim_p, xr_scr, xi_scr, re_p, im_p,
                 nb=BATCH, tt=PROMPT_TT)

    @pl.when(i == N_PROMPT_TILES)
    def _():
        _s5_path(True, *common, lambda: sre_ref[...], lambda: sim_ref[...], g_ref, nre_s, nim_s,
                 xr_scr, xi_scr, re_s, im_s, nb=DEC_BATCH, tt=DEC_SEQ)


def _s5_mix(hn, bb, ab, cre, cim, d, sre, sim, layer):
    o = layer
    nst = SSM_GROUPS * SSM_STATE
    blk = lambda k, i: (k, 0, 0)
    st_spec = lambda n: pl.BlockSpec((n, SSM_BLK_ST), lambda k, i: (0, k))
    return pl.pallas_call(
        _s5_body,
        out_shape=(
            jax.ShapeDtypeStruct((M, D_MODEL), BF16),
            jax.ShapeDtypeStruct((BATCH, nst), F32), jax.ShapeDtypeStruct((BATCH, nst), F32),
            jax.ShapeDtypeStruct((DEC_BATCH, nst), F32), jax.ShapeDtypeStruct((DEC_BATCH, nst), F32),
        ),
        grid=(N_SSM_BLK, N_TILES),
        in_specs=[
            pl.BlockSpec((ROW_TILE, SSM_BLK_IN), lambda k, i: (i, k)),
            pl.BlockSpec((None, SSM_BLK_IN, 2 * SSM_BLK_ST), blk),
            pl.BlockSpec((None, 2, SSM_BLK_ST), blk),
            pl.BlockSpec((None, None, SSM_BLK_ST, SSM_BLK_IN), lambda k, i: (o, k, 0, 0)),
            pl.BlockSpec((None, None, SSM_BLK_ST, SSM_BLK_IN), lambda k, i: (o, k, 0, 0)),
            pl.BlockSpec((None, 1, SSM_BLK_IN), lambda k, i: (o, 0, k)),
            pl.BlockSpec((None, DEC_BATCH, SSM_BLK_ST), lambda k, i: (o, 0, k)),
            pl.BlockSpec((None, DEC_BATCH, SSM_BLK_ST), lambda k, i: (o, 0, k)),
        ],
        out_specs=(
            pl.BlockSpec((ROW_TILE, SSM_BLK_IN), lambda k, i: (i, k)),
            st_spec(BATCH), st_spec(BATCH), st_spec(DEC_BATCH), st_spec(DEC_BATCH),
        ),
        scratch_shapes=[
            pltpu.VMEM((ROW_TILE, SSM_BLK_ST), F32), pltpu.VMEM((ROW_TILE, SSM_BLK_ST), F32),
            pltpu.VMEM((SUBLANES, SSM_BLK_ST), F32), pltpu.VMEM((SUBLANES, SSM_BLK_ST), F32),
            pltpu.VMEM((DEC_BATCH, SSM_BLK_ST), F32), pltpu.VMEM((DEC_BATCH, SSM_BLK_ST), F32),
        ],
        compiler_params=pltpu.CompilerParams(
            dimension_semantics=("arbitrary", "arbitrary"), vmem_limit_bytes=VMEM_LIMIT),
        name="s5_mix",
    )(hn, bb, ab, cre, cim, d, sre, sim)


def _time_major(a, axis):
    a = jnp.swapaxes(a, axis, axis + 1)
    return a.reshape(a.shape[:axis] + (a.shape[axis] * a.shape[axis + 1],) + a.shape[axis + 2:])


def _batch_major(a, nb):
    n, r, ch = a.shape
    return jnp.swapaxes(a.reshape(n, r // nb, nb, ch), 1, 2)


def _block_diag(a):
    n, k, g, r, c = a.shape
    eye = jnp.eye(g, dtype=jnp.bool_)[None, None, :, None, :, None]
    out = jnp.where(eye, a[:, :, :, :, None, :], jnp.zeros((), a.dtype))
    return out.reshape(n, k, g * r, g * c)


def _pad_ff(w, axis):
    pad = [(0, 0)] * w.ndim
    pad[axis] = (0, D_FF_PAD - D_FF)
    return jnp.pad(w, pad).astype(BF16)


def kernel(x_prompt, x_sample, state_pool, state_conv, state_rglru, state_ssm_re, state_ssm_im, norm_ffn1, w_ffn1_in, w_ffn1_out, norm_mix, norm_ffn2, w_ffn2_in, w_ffn2_out, w_in_even, pool_w, pool_scale, conv_w, conv_b, gate_a_w, gate_a_b, gate_x_w, gate_x_b, rglru_lambda, w_out_even, ssm_lambda_re, ssm_lambda_im, ssm_log_step, ssm_b_re, ssm_b_im, ssm_c_re, ssm_c_im, ssm_d, w_glu, final_norm):
    x = jnp.concatenate([_time_major(x_prompt, 0), _time_major(x_sample, 0)], axis=0)

    ffn_w = []
    for w_in, w_out in ((w_ffn1_in, w_ffn1_out), (w_ffn2_in, w_ffn2_out)):
        ffn_w.append((_pad_ff(w_in[..., :D_FF], 2), _pad_ff(w_in[..., D_FF:], 2), _pad_ff(w_out, 1)))
    w_in_even_b = w_in_even.astype(BF16)
    w_out_even_b = w_out_even.astype(BF16)
    pool_w_b = pool_w.astype(BF16)
    gate_a_b16 = gate_a_w.astype(BF16)
    gate_x_b16 = gate_x_w.astype(BF16)
    w_glu_b = w_glu.astype(BF16)
    vec3 = lambda a: a[:, None, :]

    spool = _time_major(state_pool, 1)
    sconv = _time_major(state_conv, 1)
    ssm_re = state_ssm_re.reshape(N_ODD, DEC_BATCH, SSM_GROUPS * SSM_STATE)
    ssm_im = state_ssm_im.reshape(N_ODD, DEC_BATCH, SSM_GROUPS * SSM_STATE)

    blk_row = lambda a: a.reshape(N_ODD, N_SSM_BLK, 1, SSM_BLK_ST)
    lam_re = blk_row(ssm_lambda_re)
    lam_im = blk_row(ssm_lambda_im)
    log_step = blk_row(jnp.broadcast_to(ssm_log_step[:, :, None], (N_ODD, SSM_GROUPS, SSM_STATE)))
    grp = lambda a: a.reshape(N_ODD, N_SSM_BLK, SSM_BLK_GROUPS, a.shape[2], a.shape[3])
    b_re_m = _block_diag(jnp.swapaxes(grp(ssm_b_re), 3, 4))
    b_im_m = _block_diag(jnp.swapaxes(grp(ssm_b_im), 3, 4))
    c_re_m = _block_diag(jnp.swapaxes(grp(ssm_c_re), 3, 4)).astype(BF16)
    c_im_m = _block_diag(jnp.swapaxes(grp(ssm_c_im), 3, 4)).astype(BF16)

    outs = dict(pool_p=[], pool_s=[], conv_p=[], conv_s=[], h_p=[], h_s=[], re_p=[], re_s=[], im_p=[], im_s=[])

    xo = x
    for layer in range(DEPTH):
        row = slice(layer, layer + 1)
        xo, hn = _ffn(xo, norm_ffn1[row], *ffn_w[0], norm_mix[row], layer)
        if layer % 2 == 0:
            e = layer // 2
            z = _inproj(hn, w_in_even_b, e)
            p_out, r_out, np_p, np_s, nc_p, nc_s, nh_p, nh_s = _even_mix(
                z, pool_w_b, vec3(pool_scale), conv_w, vec3(conv_b), gate_a_b16, vec3(gate_a_b),
                gate_x_b16, vec3(gate_x_b), vec3(rglru_lambda), spool, sconv, state_rglru, e)
            xo = _outproj(xo, p_out, r_out, w_out_even_b, e)
            for k, v in (("pool_p", np_p), ("pool_s", np_s), ("conv_p", nc_p), ("conv_s", nc_s),
                         ("h_p", nh_p), ("h_s", nh_s)):
                outs[k].append(v)
        else:
            o = layer // 2
            ab, bb = _s5_params(lam_re[o], lam_im[o], log_step[o], b_re_m[o], b_im_m[o])
            g, re_p, im_p, re_s, im_s = _s5_mix(hn, bb, ab, c_re_m, c_im_m, vec3(ssm_d), ssm_re, ssm_im, o)
            xo = _glu(xo, g, w_glu_b, o)
            for k, v in (("re_p", re_p), ("im_p", im_p), ("re_s", re_s), ("im_s", im_s)):
                outs[k].append(v)
        last = layer == DEPTH - 1
        (xo,) = _ffn(xo, norm_ffn2[row], *ffn_w[1], final_norm[None, :], layer, emit_x=not last, emit_norm=last)

    y = xo
    y_prompt = jnp.swapaxes(y[:MP].reshape(SEQ, BATCH, D_MODEL), 0, 1)
    y_sample = jnp.swapaxes(y[MP:].reshape(DEC_SEQ, DEC_BATCH, D_MODEL), 0, 1)
    st = lambda k: jnp.stack(outs[k])
    ssm = lambda k, nb: st(k).reshape(N_ODD, nb, SSM_GROUPS, SSM_STATE)
    return (y_prompt, y_sample,
            _batch_major(st("pool_p"), BATCH), _batch_major(st("pool_s"), DEC_BATCH),
            _batch_major(st("conv_p"), BATCH), _batch_major(st("conv_s"), DEC_BATCH),
            st("h_p"), st("h_s"),
            ssm("re_p", BATCH), ssm("re_s", DEC_BATCH), ssm("im_p", BATCH), ssm("im_s", DEC_BATCH))
```
